```python
import math
import jax, jax.numpy as jnp
from jax import lax
import numpy as np

D_MODEL = 1024
BATCH = 8
SEQ = 2048
DEPTH = 4

MEM_LEN = 256
N_MIXERS = 2
N_CONV_LAYERS = (DEPTH + 1) // 2
N_POOL_LAYERS = DEPTH // 2
N_XHEADS = 4
XHEAD_DIM = D_MODEL // N_XHEADS
D_FF = 4 * D_MODEL
CONV_WIDTH = 31
POOL_WINDOWS = (2, 4, 8, 16)
N_POOL_GROUPS = len(POOL_WINDOWS)
POOL_GROUP_DIM = D_MODEL // N_POOL_GROUPS
RMS_EPS = 1e-6
LN_EPS = 1e-5

kernel_name = "hybrid_conv_pool_memxattn_trunk"


def rmsnorm(x, g):
    xf = x.astype(jnp.float32)
    y = xf * lax.rsqrt(jnp.mean(xf * xf, axis=-1, keepdims=True) + RMS_EPS)
    return (y * g.astype(jnp.float32)).astype(x.dtype)


def layernorm(x, g, b):
    xf = x.astype(jnp.float32)
    mu = jnp.mean(xf, axis=-1, keepdims=True)
    var = jnp.mean(jnp.square(xf - mu), axis=-1, keepdims=True)
    y = (xf - mu) * lax.rsqrt(var + LN_EPS)
    return (y * g.astype(jnp.float32) + b.astype(jnp.float32)).astype(x.dtype)


def conv_mixer(h, w_in, b_in, w_dw, b_dw, ln_g, ln_b, w_out, b_out):
    u = h @ w_in + b_in
    a, gate = jnp.split(u, 2, axis=-1)
    u = a * jax.nn.sigmoid(gate)
    u = lax.conv_general_dilated(
        u, w_dw[:, None, :].astype(u.dtype),
        window_strides=(1,), padding=[(CONV_WIDTH - 1, 0)],
        dimension_numbers=("NWC", "WIO", "NWC"),
        feature_group_count=D_MODEL) + b_dw
    u = jax.nn.silu(layernorm(u, ln_g, ln_b))
    return u @ w_out + b_out


def pool_mixer(h, w_pool, scale):
    B, S, D = h.shape
    hf = h.astype(jnp.float32)
    cs = jnp.cumsum(hf, axis=1)
    count = jnp.arange(1, S + 1, dtype=jnp.float32)
    groups = []
    for g, w in enumerate(POOL_WINDOWS):
        sl = slice(g * POOL_GROUP_DIM, (g + 1) * POOL_GROUP_DIM)
        c = cs[..., sl]
        lagged = jnp.pad(c, ((0, 0), (w, 0), (0, 0)))[:, :S]
        mean = (c - lagged) / jnp.minimum(count, float(w))[None, :, None]
        groups.append(mean - hf[..., sl])
    p = jnp.stack(groups, axis=2).astype(h.dtype)
    y = jnp.einsum("bsgc,gcd->bsgd", p, w_pool).reshape(B, S, D)
    return y * scale


def mem_cross_attn(h, memn, wq, wk, wv, wo):
    B, S, D = h.shape
    q = (h @ wq).reshape(B, S, N_XHEADS, XHEAD_DIM)
    k = (memn @ wk).reshape(B, MEM_LEN, N_XHEADS, XHEAD_DIM)
    v = (memn @ wv).reshape(B, MEM_LEN, N_XHEADS, XHEAD_DIM)
    s = jnp.einsum("bshd,bmhd->bhsm", q, k).astype(jnp.float32) * (1.0 / math.sqrt(XHEAD_DIM))
    p = jax.nn.softmax(s, axis=-1).astype(v.dtype)
    o = jnp.einsum("bhsm,bmhd->bshd", p, v).reshape(B, S, D)
    return o @ wo


def sqrelu_mlp(h, w1, w2):
    return jnp.square(jax.nn.relu(h @ w1)) @ w2


def setup_inputs(seed: int = 0) -> dict:
    key = jax.random.key(seed)
    ks = jax.random.split(key, 24)
    D = D_MODEL
    nrm = lambda k, shape, fan_in: jax.random.normal(k, shape, jnp.float32) * (fan_in ** -0.5)
    gain = lambda k, shape: 1.0 + 0.05 * jax.random.normal(k, shape, jnp.float32)
    small = lambda k, shape: 0.02 * jax.random.normal(k, shape, jnp.float32)
    return {
        "x": jax.random.normal(ks[0], (BATCH, SEQ, D), jnp.float32),
        "mem": jax.random.normal(ks[1], (BATCH, MEM_LEN, D), jnp.float32),
        "mem_norm": gain(ks[2], (D,)),
        "norm_mix": gain(ks[3], (DEPTH, D)),
        "norm_xattn": gain(ks[4], (DEPTH, D)),
        "norm_mlp": gain(ks[5], (DEPTH, D)),
        "conv_w_in": nrm(ks[6], (N_CONV_LAYERS, D, 2 * D), D),
        "conv_b_in": small(ks[7], (N_CONV_LAYERS, 2 * D)),
        "conv_w_dw": nrm(ks[8], (N_CONV_LAYERS, CONV_WIDTH, D), CONV_WIDTH),
        "conv_b_dw": small(ks[9], (N_CONV_LAYERS, D)),
        "conv_ln_g": gain(ks[10], (N_CONV_LAYERS, D)),
        "conv_ln_b": small(ks[11], (N_CONV_LAYERS, D)),
        "conv_w_out": nrm(ks[12], (N_CONV_LAYERS, D, D), D),
        "conv_b_out": small(ks[13], (N_CONV_LAYERS, D)),
        "pool_w": nrm(ks[14], (N_POOL_LAYERS, N_POOL_GROUPS, POOL_GROUP_DIM, POOL_GROUP_DIM), POOL_GROUP_DIM),
        "pool_scale": gain(ks[15], (N_POOL_LAYERS, D)),
        "xattn_wq": nrm(ks[16], (DEPTH, D, D), D),
        "xattn_wk": nrm(ks[17], (DEPTH, D, D), D),
        "xattn_wv": nrm(ks[18], (DEPTH, D, D), D),
        "xattn_wo": nrm(ks[19], (DEPTH, D, D), D),
        "mlp_w1": nrm(ks[20], (DEPTH, D, D_FF), D),
        "mlp_w2": nrm(ks[21], (DEPTH, D_FF, D), D_FF),
        "final_norm": gain(ks[22], (D,)),
    }


def reference(x, mem, mem_norm, norm_mix, norm_xattn, norm_mlp,
              conv_w_in, conv_b_in, conv_w_dw, conv_b_dw, conv_ln_g, conv_ln_b,
              conv_w_out, conv_b_out, pool_w, pool_scale,
              xattn_wq, xattn_wk, xattn_wv, xattn_wo, mlp_w1, mlp_w2, final_norm):
    memn = rmsnorm(mem, mem_norm)
    for i in range(DEPTH):
        j = i // N_MIXERS
        h = rmsnorm(x, norm_mix[i])
        if i % N_MIXERS == 0:
            x = x + conv_mixer(h, conv_w_in[j], conv_b_in[j], conv_w_dw[j], conv_b_dw[j],
                               conv_ln_g[j], conv_ln_b[j], conv_w_out[j], conv_b_out[j])
        else:
            x = x + pool_mixer(h, pool_w[j], pool_scale[j])
        x = x + mem_cross_attn(rmsnorm(x, norm_xattn[i]), memn,
                               xattn_wq[i], xattn_wk[i], xattn_wv[i], xattn_wo[i])
        x = x + sqrelu_mlp(rmsnorm(x, norm_mlp[i]), mlp_w1[i], mlp_w2[i])
    return rmsnorm(x, final_norm)
```

```python
import functools
import math

import jax
import jax.numpy as jnp
from jax import lax
from jax.experimental import pallas as pl
from jax.experimental.pallas import tpu as pltpu

N_XHEADS = 4
CONV_WIDTH = 31
POOL_WINDOWS = (2, 4, 8, 16)
RMS_EPS = 1e-6
LN_EPS = 1e-5

CONV_HALO = 32
POOL_HALO = 16
ROW_CHUNK = 64
LANES = 128

VMEM_LIMIT_BYTES = 56 * 1024 * 1024

_F32 = jnp.float32
_BF16 = jnp.bfloat16


def _dot(a, b):
    return jnp.dot(a, b, preferred_element_type=_F32)


def _rmsnorm(x, g):
    ms = jnp.mean(x * x, axis=-1, keepdims=True)
    return x * lax.rsqrt(ms + RMS_EPS) * g


def _const_spec(shape):
    return pl.BlockSpec(shape, lambda *_: (0,) * len(shape),
                        pipeline_mode=pl.Buffered(1))


def _params(n_grid_axes):
    return pltpu.CompilerParams(
        dimension_semantics=("arbitrary",) * n_grid_axes,
        vmem_limit_bytes=VMEM_LIMIT_BYTES)


def _kv_kernel(mem_ref, g_ref, wk_ref, wv_ref, k_ref, v_ref):
    memn = _rmsnorm(mem_ref[...], g_ref[...]).astype(_BF16)
    k_ref[...] = _dot(memn, wk_ref[...]).astype(_BF16)
    v_ref[...] = _dot(memn, wv_ref[...]).astype(_BF16)


def _kv_proj(mem, mem_norm, wk, wv):
    depth, d, _ = wk.shape
    b, m, _ = mem.shape
    out = jax.ShapeDtypeStruct((depth, b, m, d), _BF16)
    return pl.pallas_call(
        _kv_kernel,
        grid=(depth, b),
        in_specs=[
            pl.BlockSpec((None, m, d), lambda l, i: (i, 0, 0)),
            pl.BlockSpec((1, d), lambda l, i: (0, 0)),
            pl.BlockSpec((None, d, d), lambda l, i: (l, 0, 0)),
            pl.BlockSpec((None, d, d), lambda l, i: (l, 0, 0)),
        ],
        out_specs=[
            pl.BlockSpec((None, None, m, d), lambda l, i: (l, i, 0, 0)),
            pl.BlockSpec((None, None, m, d), lambda l, i: (l, i, 0, 0)),
        ],
        out_shape=[out, out],
        compiler_params=_params(2),
        name="kv_proj",
    )(mem, mem_norm.reshape(1, d), wk, wv)


def _xattn(x, g, wq_ref, k_ref, v_ref, wo_ref):
    d = x.shape[-1]
    dh = d // N_XHEADS
    scale = 1.0 / math.sqrt(dh)
    hn = _rmsnorm(x, g).astype(_BF16)
    q = _dot(hn, wq_ref[...])
    heads = []
    for h in range(N_XHEADS):
        cols = slice(h * dh, (h + 1) * dh)
        qh = q[:, cols].astype(_BF16)
        s = lax.dot_general(qh, k_ref[:, cols], (((1,), (1,)), ((), ())),
                            preferred_element_type=_F32) * scale
        e = jnp.exp(s - jnp.max(s, axis=-1, keepdims=True))
        p = e / jnp.sum(e, axis=-1, keepdims=True)
        heads.append(_dot(p.astype(_BF16), v_ref[:, cols]).astype(_BF16))
    o = jnp.concatenate(heads, axis=-1)
    return x + _dot(o, wo_ref[...])


def _xattn_kernel(x_ref, g_ref, wq_ref, k_ref, v_ref, wo_ref, o_ref):
    o_ref[...] = _xattn(x_ref[...], g_ref[...], wq_ref, k_ref, v_ref, wo_ref)


def _xattn_specs(layer, t, d, m):
    x_spec = pl.BlockSpec((None, t, d), lambda b, s: (b, s, 0))
    kv_spec = pl.BlockSpec((None, None, m, d), lambda b, s: (layer, b, 0, 0))
    return x_spec, kv_spec


def _xattn_call(x, g, wq, k_all, v_all, wo, layer, t):
    b, s, d = x.shape
    m = k_all.shape[2]
    x_spec, kv_spec = _xattn_specs(layer, t, d, m)
    return pl.pallas_call(
        _xattn_kernel,
        grid=(b, s // t),
        in_specs=[x_spec, _const_spec((1, d)), _const_spec((d, d)),
                  kv_spec, kv_spec, _const_spec((d, d))],
        out_specs=x_spec,
        out_shape=jax.ShapeDtypeStruct(x.shape, x.dtype),
        compiler_params=_params(2),
        name="xattn",
    )(x, g.reshape(1, d), wq, k_all, v_all, wo)


def _pool_xattn_kernel(x_ref, gm_ref, pw_ref, ps_ref, gx_ref, wq_ref, k_ref,
                       v_ref, wo_ref, o_ref, hext_ref, p_ref):
    t, d = x_ref.shape
    gd = d // len(POOL_WINDOWS)
    s_idx = pl.program_id(1)

    @pl.when(s_idx == 0)
    def _():
        hext_ref[:, 0:POOL_HALO, :] = jnp.zeros((d // LANES, POOL_HALO, LANES), _F32)

    @pl.when(s_idx > 0)
    def _():
        hext_ref[:, 0:POOL_HALO, :] = hext_ref[:, t:t + POOL_HALO, :]

    x = x_ref[...]
    h = _rmsnorm(x, gm_ref[...])
    for c in range(d // LANES):
        hext_ref[c, POOL_HALO:POOL_HALO + t, :] = h[:, c * LANES:(c + 1) * LANES]

    def chunk(i, carry):
        r0 = pl.multiple_of(i * ROW_CHUNK, ROW_CHUNK)
        pos = (s_idx * t + r0 + 1
               + lax.broadcasted_iota(jnp.int32, (ROW_CHUNK, LANES), 0))
        for c in range(d // LANES):
            w = POOL_WINDOWS[c * LANES // gd]
            cur = hext_ref[c, pl.ds(POOL_HALO + r0, ROW_CHUNK), :]
            acc = cur
            for j in range(1, w):
                acc = acc + hext_ref[c, pl.ds(POOL_HALO + r0 - j, ROW_CHUNK), :]
            cnt = jnp.minimum(pos, w).astype(_F32)
            p_ref[pl.ds(r0, ROW_CHUNK), c * LANES:(c + 1) * LANES] = (
                acc / cnt - cur).astype(_BF16)
        return carry

    lax.fori_loop(0, t // ROW_CHUNK, chunk, 0)

    ys = []
    for g in range(len(POOL_WINDOWS)):
        cols = slice(g * gd, (g + 1) * gd)
        ys.append(_dot(p_ref[:, cols], pw_ref[g]))
    x = x + jnp.concatenate(ys, axis=-1) * ps_ref[...]
    o_ref[...] = _xattn(x, gx_ref[...], wq_ref, k_ref, v_ref, wo_ref)


def _pool_xattn_call(x, gm, pool_w, pool_scale, gx, wq, k_all, v_all, wo,
                     layer, t):
    b, s, d = x.shape
    m = k_all.shape[2]
    ng, gd, _ = pool_w.shape
    x_spec, kv_spec = _xattn_specs(layer, t, d, m)
    return pl.pallas_call(
        _pool_xattn_kernel,
        grid=(b, s // t),
        in_specs=[x_spec, _const_spec((1, d)), _const_spec((ng, gd, gd)),
                  _const_spec((1, d)), _const_spec((1, d)),
                  _const_spec((d, d)), kv_spec, kv_spec, _const_spec((d, d))],
        out_specs=x_spec,
        out_shape=jax.ShapeDtypeStruct(x.shape, x.dtype),
        scratch_shapes=[pltpu.VMEM((d // LANES, POOL_HALO + t, LANES), _F32),
                        pltpu.VMEM((t, d), _BF16)],
        compiler_params=_params(2),
        name="pool_xattn",
    )(x, gm.reshape(1, d), pool_w, pool_scale.reshape(1, d), gx.reshape(1, d),
      wq, k_all, v_all, wo)


def _conv_kernel(x_ref, g_ref, win_ref, bin_ref, wdw_ref, bdw_ref, lng_ref,
                 lnb_ref, wout_ref, bout_ref, o_ref, uext_ref, c_ref):
    t, d = x_ref.shape
    s_idx = pl.program_id(1)

    @pl.when(s_idx == 0)
    def _():
        uext_ref[:, 0:CONV_HALO, :] = jnp.zeros((d // LANES, CONV_HALO, LANES), _F32)

    @pl.when(s_idx > 0)
    def _():
        uext_ref[:, 0:CONV_HALO, :] = uext_ref[:, t:t + CONV_HALO, :]

    x = x_ref[...]
    hn = _rmsnorm(x, g_ref[...]).astype(_BF16)
    u = _dot(hn, win_ref[...]) + bin_ref[...]
    glu = u[:, :d] * jax.nn.sigmoid(u[:, d:])
    for c in range(d // LANES):
        uext_ref[c, CONV_HALO:CONV_HALO + t, :] = glu[:, c * LANES:(c + 1) * LANES]

    first = CONV_HALO - (CONV_WIDTH - 1)
    sub = ROW_CHUNK // 8

    def chunk(i, carry):
        r0 = pl.multiple_of(i * ROW_CHUNK, ROW_CHUNK)
        for c in range(d // LANES):
            cols = slice(c * LANES, (c + 1) * LANES)
            acc = jnp.zeros((sub, 8, LANES), _F32)
            for k in range(CONV_WIDTH):
                rows = uext_ref[c, pl.ds(r0 + first + k, ROW_CHUNK), :]
                acc = acc + rows.reshape(sub, 8, LANES) * wdw_ref[k, :, cols]
            c_ref[pl.ds(r0, ROW_CHUNK), cols] = acc.reshape(ROW_CHUNK, LANES)
        return carry

    lax.fori_loop(0, t // ROW_CHUNK, chunk, 0)

    c = c_ref[...] + bdw_ref[...]
    mu = jnp.mean(c, axis=-1, keepdims=True)
    cc = c - mu
    var = jnp.mean(cc * cc, axis=-1, keepdims=True)
    y = cc * lax.rsqrt(var + LN_EPS) * lng_ref[...] + lnb_ref[...]
    y = (y * jax.nn.sigmoid(y)).astype(_BF16)
    o_ref[...] = x + _dot(y, wout_ref[...]) + bout_ref[...]


def _conv_call(x, g, w_in, b_in, w_dw, b_dw, ln_g, ln_b, w_out, b_out, t):
    b, s, d = x.shape
    x_spec = pl.BlockSpec((None, t, d), lambda i, j: (i, j, 0))
    w_dw_rows = jnp.broadcast_to(w_dw[:, None, :], (CONV_WIDTH, 8, d))
    return pl.pallas_call(
        _conv_kernel,
        grid=(b, s // t),
        in_specs=[x_spec, _const_spec((1, d)), _const_spec((d, 2 * d)),
                  _const_spec((1, 2 * d)), _const_spec((CONV_WIDTH, 8, d)),
                  _const_spec((1, d)), _const_spec((1, d)), _const_spec((1, d)),
                  _const_spec((d, d)), _const_spec((1, d))],
        out_specs=x_spec,
        out_shape=jax.ShapeDtypeStruct(x.shape, x.dtype),
        scratch_shapes=[pltpu.VMEM((d // LANES, CONV_HALO + t, LANES), _F32),
                        pltpu.VMEM((t, d), _F32)],
        compiler_params=_params(2),
        name="conv_mixer",
    )(x, g.reshape(1, d), w_in, b_in.reshape(1, 2 * d), w_dw_rows,
      b_dw.reshape(1, d), ln_g.reshape(1, d), ln_b.reshape(1, d), w_out,
      b_out.reshape(1, d))


def _mlp_kernel(x_ref, g_ref, w1_ref, w2_ref, gf_ref, o_ref, *, ff_chunk,
                final_norm):
    x = x_ref[...]
    hn = _rmsnorm(x, g_ref[...]).astype(_BF16)
    d_ff = w1_ref.shape[1]
    acc = x
    for c in range(d_ff // ff_chunk):
        cols = slice(c * ff_chunk, (c + 1) * ff_chunk)
        a = jnp.maximum(_dot(hn, w1_ref[:, cols]), 0.0)
        acc = acc + _dot((a * a).astype(_BF16), w2_ref[cols, :])
    if final_norm:
        acc = _rmsnorm(acc, gf_ref[...])
    o_ref[...] = acc


def _mlp_call(x, g, w1, w2, final_g, t, ff_chunk, final_norm):
    b, s, d = x.shape
    d_ff = w1.shape[1]
    x_spec = pl.BlockSpec((None, t, d), lambda i, j: (i, j, 0))
    return pl.pallas_call(
        functools.partial(_mlp_kernel, ff_chunk=ff_chunk, final_norm=final_norm),
        grid=(b, s // t),
        in_specs=[x_spec, _const_spec((1, d)), _const_spec((d, d_ff)),
                  _const_spec((d_ff, d)), _const_spec((1, d))],
        out_specs=x_spec,
        out_shape=jax.ShapeDtypeStruct(x.shape, x.dtype),
        compiler_params=_params(2),
        name="mlp_final" if final_norm else "mlp",
    )(x, g.reshape(1, d), w1, w2, final_g.reshape(1, d))


def kernel(x, mem, mem_norm, norm_mix, norm_xattn, norm_mlp, conv_w_in, conv_b_in, conv_w_dw, conv_b_dw, conv_ln_g, conv_ln_b, conv_w_out, conv_b_out, pool_w, pool_scale, xattn_wq, xattn_wk, xattn_wv, xattn_wo, mlp_w1, mlp_w2, final_norm):
    depth = norm_mix.shape[0]
    bf = lambda w: w.astype(_BF16)
    conv_w_in, conv_w_out, pool_w = bf(conv_w_in), bf(conv_w_out), bf(pool_w)
    xattn_wq, xattn_wo = bf(xattn_wq), bf(xattn_wo)
    mlp_w1, mlp_w2 = bf(mlp_w1), bf(mlp_w2)

    k_all, v_all = _kv_proj(mem, mem_norm, bf(xattn_wk), bf(xattn_wv))

    for i in range(depth):
        j = i // 2
        if i % 2 == 0:
            x = _conv_call(x, norm_mix[i], conv_w_in[j], conv_b_in[j],
                           conv_w_dw[j], conv_b_dw[j], conv_ln_g[j],
                           conv_ln_b[j], conv_w_out[j], conv_b_out[j], t=512)
            x = _xattn_call(x, norm_xattn[i], xattn_wq[i], k_all, v_all,
                            xattn_wo[i], layer=i, t=512)
        else:
            x = _pool_xattn_call(x, norm_mix[i], pool_w[j], pool_scale[j],
                                 norm_xattn[i], xattn_wq[i], k_all, v_all,
                                 xattn_wo[i], layer=i, t=512)
        x = _mlp_call(x, norm_mlp[i], mlp_w1[i], mlp_w2[i], final_norm,
                      t=512, ff_chunk=1024, final_norm=(i == depth - 1))
    return x
```
